```python
import jax, jax.numpy as jnp
from jax import lax
import numpy as np

D_MODEL = 2048
BATCH = 4
SEQ = 4096
DEPTH = 4

N_MIXERS = 2
N_CONV_LAYERS = (DEPTH + 1) // 2
N_POOL_LAYERS = DEPTH // 2
CONV_WIDTH = 3
POOL_WINDOWS = (2, 4, 8, 16)
N_POOL_GROUPS = len(POOL_WINDOWS)
GROUP_DIM = D_MODEL // N_POOL_GROUPS
D_FF = 5632
N_MOD = 9
EPS = 1e-6
FFN_RES_WEIGHT = 0.5

kernel_name = "hybrid_shortconv_pool_macaron_adaln"


def rmsnorm(x, g):
    xf = x.astype(jnp.float32)
    y = xf * lax.rsqrt(jnp.mean(xf * xf, axis=-1, keepdims=True) + EPS)
    return (y * g.astype(jnp.float32)).astype(x.dtype)


def modulate(h, shift, scale):
    return h * (1 + scale[:, None, :]) + shift[:, None, :]


def swiglu(h, w_in, w_out):
    gu = jnp.einsum('bsd,df->bsf', h, w_in)
    g, u = jnp.split(gu, 2, axis=-1)
    return jnp.einsum('bsf,fd->bsd', jax.nn.silu(g) * u, w_out)


def causal_shift(u, k):
    pad = [(0, 0)] * u.ndim
    pad[1] = (k, 0)
    return jnp.pad(u, pad)[:, :u.shape[1]]


def short_conv_mixer(h, w_in, w_conv, w_out):
    proj = jnp.einsum('bsd,de->bse', h, w_in)
    b_gate, c_gate, v = jnp.split(proj, 3, axis=-1)
    u = c_gate * v
    y = sum(w_conv[CONV_WIDTH - 1 - k] * causal_shift(u, k) for k in range(CONV_WIDTH))
    return jnp.einsum('bsd,de->bse', b_gate * y, w_out)


def pooling_mixer(h, w_group, scale):
    B, S, D = h.shape
    hg = h.reshape(B, S, N_POOL_GROUPS, GROUP_DIM).astype(jnp.float32)
    cs = jnp.cumsum(hg, axis=1)
    t = jnp.arange(S)
    outs = []
    for g, w in enumerate(POOL_WINDOWS):
        csg = cs[:, :, g]
        win_sum = csg - causal_shift(csg, w)
        count = jnp.minimum(t + 1, w).astype(jnp.float32)[None, :, None]
        outs.append(win_sum / count - hg[:, :, g])
    pooled = jnp.stack(outs, axis=2).astype(h.dtype)
    y = jnp.einsum('bsgc,gce->bsge', pooled, w_group).reshape(B, S, D)
    return y * scale


def setup_inputs(seed: int = 0) -> dict:
    key = jax.random.key(seed)
    ks = jax.random.split(key, 20)
    D, F = D_MODEL, D_FF
    f32 = jnp.float32

    def nrm(k, shape, std):
        return jax.random.normal(k, shape, f32) * std

    return {
        "x": nrm(ks[0], (BATCH, SEQ, D), 1.0),
        "c": nrm(ks[1], (BATCH, D), 1.0),
        "norm_ffn1": 1.0 + nrm(ks[2], (DEPTH, D), 0.05),
        "norm_mix": 1.0 + nrm(ks[3], (DEPTH, D), 0.05),
        "norm_ffn2": 1.0 + nrm(ks[4], (DEPTH, D), 0.05),
        "w_ada": nrm(ks[5], (DEPTH, D, N_MOD * D), 0.5 * D ** -0.5),
        "b_ada": nrm(ks[6], (DEPTH, N_MOD * D), 0.02),
        "w_ffn1_in": nrm(ks[7], (DEPTH, D, 2 * F), D ** -0.5),
        "w_ffn1_out": nrm(ks[8], (DEPTH, F, D), F ** -0.5),
        "w_ffn2_in": nrm(ks[9], (DEPTH, D, 2 * F), D ** -0.5),
        "w_ffn2_out": nrm(ks[10], (DEPTH, F, D), F ** -0.5),
        "conv_in": nrm(ks[11], (N_CONV_LAYERS, D, 3 * D), D ** -0.5),
        "conv_w": nrm(ks[12], (N_CONV_LAYERS, CONV_WIDTH, D), CONV_WIDTH ** -0.5),
        "conv_out": nrm(ks[13], (N_CONV_LAYERS, D, D), D ** -0.5),
        "pool_w": nrm(ks[14], (N_POOL_LAYERS, N_POOL_GROUPS, GROUP_DIM, GROUP_DIM), GROUP_DIM ** -0.5),
        "pool_scale": 1.0 + nrm(ks[15], (N_POOL_LAYERS, D), 0.1),
        "final_norm": 1.0 + nrm(ks[16], (D,), 0.05),
    }


def reference(x, c, norm_ffn1, norm_mix, norm_ffn2, w_ada, b_ada, w_ffn1_in, w_ffn1_out,
              w_ffn2_in, w_ffn2_out, conv_in, conv_w, conv_out, pool_w, pool_scale, final_norm):
    c_act = jax.nn.silu(c)
    for i in range(DEPTH):
        mods = jnp.einsum('bd,de->be', c_act, w_ada[i]) + b_ada[i]
        sh1, sc1, g1, sh2, sc2, g2, sh3, sc3, g3 = jnp.split(mods, N_MOD, axis=-1)

        h = modulate(rmsnorm(x, norm_ffn1[i]), sh1, sc1)
        x = x + FFN_RES_WEIGHT * g1[:, None, :] * swiglu(h, w_ffn1_in[i], w_ffn1_out[i])

        h = modulate(rmsnorm(x, norm_mix[i]), sh2, sc2)
        j = i // N_MIXERS
        if i % N_MIXERS == 0:
            m = short_conv_mixer(h, conv_in[j], conv_w[j], conv_out[j])
        else:
            m = pooling_mixer(h, pool_w[j], pool_scale[j])
        x = x + g2[:, None, :] * m

        h = modulate(rmsnorm(x, norm_ffn2[i]), sh3, sc3)
        x = x + FFN_RES_WEIGHT * g3[:, None, :] * swiglu(h, w_ffn2_in[i], w_ffn2_out[i])
    return rmsnorm(x, final_norm)
```

```python
import functools

import jax
import jax.numpy as jnp
from jax import lax
from jax.experimental import pallas as pl
from jax.experimental.pallas import tpu as pltpu

D_MODEL = 2048
DEPTH = 4
N_MIXERS = 2
CONV_WIDTH = 3
POOL_WINDOWS = (2, 4, 8, 16)
GROUP_DIM = D_MODEL // len(POOL_WINDOWS)
D_FF = 5632
N_MOD = 9
EPS = 1e-6
FFN_RES_WEIGHT = 0.5

F32 = jnp.float32
BF16 = jnp.bfloat16

SUBLANES_F32 = 8
VMEM_BYTES_V7X = 64 * 1024 * 1024

FFN_TOKENS = 512
FFN_HIDDEN = 512
CONV_TOKENS = 512
CONV_CHANNELS = 512
POOL_TOKENS = 512
ADA_COLS = 2048
ADA_ROWS = 16
POOL_HALO = 16
CONV_HALO = SUBLANES_F32


def _vmem_limit(estimate_bytes):
    return int(min(estimate_bytes * 3 // 2 + (4 << 20), VMEM_BYTES_V7X - (4 << 20)))


def _params(n_axes, estimate_bytes):
    return pltpu.CompilerParams(
        dimension_semantics=("arbitrary",) * n_axes,
        vmem_limit_bytes=_vmem_limit(estimate_bytes),
    )


def _rms(x, gain):
    ms = jnp.mean(x * x, axis=-1, keepdims=True)
    return (x * lax.rsqrt(ms + EPS)) * gain


def _norm_mod(x, gain, shift, scale):
    return _rms(x, gain) * (1.0 + scale) + shift


def _ada_kernel(c_ref, w_ref, b_ref, o_ref):
    c = c_ref[...]
    ca = (c * jax.nn.sigmoid(c)).astype(BF16)
    y = jnp.dot(ca, w_ref[...].astype(BF16), preferred_element_type=F32)
    o_ref[...] = y[: o_ref.shape[0], :] + b_ref[...]


def _ada_mods(c, w_ada, b_ada):
    batch = c.shape[0]
    n_out = w_ada.shape[-1]
    c_pad = jnp.pad(c, ((0, ADA_ROWS - batch), (0, 0)))
    b3 = b_ada.reshape(DEPTH, 1, n_out)
    est = 2 * D_MODEL * ADA_COLS * 4 + D_MODEL * ADA_COLS * 2
    return pl.pallas_call(
        _ada_kernel,
        grid=(DEPTH, n_out // ADA_COLS),
        in_specs=[
            pl.BlockSpec((ADA_ROWS, D_MODEL), lambda i, n: (0, 0)),
            pl.BlockSpec((None, D_MODEL, ADA_COLS), lambda i, n: (i, 0, n)),
            pl.BlockSpec((None, 1, ADA_COLS), lambda i, n: (i, 0, n)),
        ],
        out_specs=pl.BlockSpec((None, batch, ADA_COLS), lambda i, n: (i, 0, n)),
        out_shape=jax.ShapeDtypeStruct((DEPTH, batch, n_out), F32),
        compiler_params=_params(2, est),
        name="ada_mods",
    )(c_pad, w_ada, b3)


def _ffn_kernel(x_ref, mod_ref, gain_ref, wg_ref, wu_ref, wo_ref, *rest, mod_row, final):
    if final:
        fgain_ref, o_ref, h_ref = rest
    else:
        o_ref, h_ref = rest
    f = pl.program_id(2)

    @pl.when(f == 0)
    def _():
        shift = mod_ref[mod_row : mod_row + 1, :]
        scale = mod_ref[mod_row + 1 : mod_row + 2, :]
        h_ref[...] = _norm_mod(x_ref[...], gain_ref[...], shift, scale).astype(BF16)
        o_ref[...] = jnp.zeros_like(o_ref)

    h = h_ref[...]
    g = jnp.dot(h, wg_ref[...], preferred_element_type=F32)
    u = jnp.dot(h, wu_ref[...], preferred_element_type=F32)
    act = ((g * jax.nn.sigmoid(g)) * u).astype(BF16)
    o_ref[...] += jnp.dot(act, wo_ref[...], preferred_element_type=F32)

    @pl.when(f == pl.num_programs(2) - 1)
    def _():
        gate = mod_ref[mod_row + 2 : mod_row + 3, :]
        y = x_ref[...] + (FFN_RES_WEIGHT * gate) * o_ref[...]
        if final:
            y = _rms(y, fgain_ref[...])
        o_ref[...] = y


def _ffn(x, mods, gain, w_in, w_out, layer, mod_row, final_gain=None):
    batch, seq, d = x.shape
    ts, tf = FFN_TOKENS, FFN_HIDDEN
    nf = D_FF // tf
    final = final_gain is not None
    in_specs = [
        pl.BlockSpec((None, ts, d), lambda b, s, f: (b, s, 0)),
        pl.BlockSpec((None, N_MOD, d), lambda b, s, f: (b, 0, 0)),
        pl.BlockSpec((None, 1, d), lambda b, s, f: (layer, 0, 0)),
        pl.BlockSpec((None, d, tf), lambda b, s, f: (layer, 0, f)),
        pl.BlockSpec((None, d, tf), lambda b, s, f: (layer, 0, nf + f)),
        pl.BlockSpec((None, tf, d), lambda b, s, f: (layer, f, 0)),
    ]
    args = [x, mods, gain, w_in, w_in, w_out]
    if final:
        in_specs.append(pl.BlockSpec((1, d), lambda b, s, f: (0, 0)))
        args.append(final_gain)
    est = (4 * ts * d * 4) + ts * d * 2 + 2 * (3 * d * tf * 2) + 3 * ts * tf * 4
    return pl.pallas_call(
        functools.partial(_ffn_kernel, mod_row=mod_row, final=final),
        grid=(batch, seq // ts, nf),
        in_specs=in_specs,
        out_specs=pl.BlockSpec((None, ts, d), lambda b, s, f: (b, s, 0)),
        out_shape=jax.ShapeDtypeStruct(x.shape, F32),
        scratch_shapes=[pltpu.VMEM((ts, d), BF16)],
        compiler_params=_params(3, est),
        name=f"ffn_l{layer}_r{mod_row}",
    )(*args)


def _conv_kernel(x_ref, mod_ref, gain_ref, wb_ref, wc_ref, wv_ref, cw_ref, wo_ref,
                 o_ref, h_ref, u_ref, carry_ref):
    s = pl.program_id(1)
    j = pl.program_id(2)
    ts = x_ref.shape[0]

    @pl.when(j == 0)
    def _():
        shift = mod_ref[3:4, :]
        scale = mod_ref[4:5, :]
        h_ref[...] = _norm_mod(x_ref[...], gain_ref[...], shift, scale).astype(BF16)
        o_ref[...] = jnp.zeros_like(o_ref)

    @pl.when(s == 0)
    def _():
        carry_ref[j] = jnp.zeros(carry_ref.shape[1:], F32)

    h = h_ref[...]
    b_gate = jnp.dot(h, wb_ref[...], preferred_element_type=F32)
    c_gate = jnp.dot(h, wc_ref[...], preferred_element_type=F32)
    v = jnp.dot(h, wv_ref[...], preferred_element_type=F32)
    u = c_gate * v
    u_ref[0:CONV_HALO, :] = carry_ref[j]
    u_ref[CONV_HALO:, :] = u
    carry_ref[j] = u[ts - CONV_HALO :, :]
    y = cw_ref[CONV_WIDTH - 1 : CONV_WIDTH, :] * u
    for k in range(1, CONV_WIDTH):
        tap = cw_ref[CONV_WIDTH - 1 - k : CONV_WIDTH - k, :]
        y = y + tap * u_ref[CONV_HALO - k : CONV_HALO - k + ts, :]
    z = (b_gate * y).astype(BF16)
    o_ref[...] += jnp.dot(z, wo_ref[...], preferred_element_type=F32)

    @pl.when(j == pl.num_programs(2) - 1)
    def _():
        gate = mod_ref[5:6, :]
        o_ref[...] = x_ref[...] + gate * o_ref[...]


def _conv_mixer(x, mods, gain, w_in, w_conv, w_out, layer, j_layer):
    batch, seq, d = x.shape
    ts, tc = CONV_TOKENS, CONV_CHANNELS
    nc = d // tc
    est = (4 * ts * d * 4) + ts * d * 2 + 2 * (4 * d * tc * 2) + 5 * ts * tc * 4
    return pl.pallas_call(
        _conv_kernel,
        grid=(batch, seq // ts, nc),
        in_specs=[
            pl.BlockSpec((None, ts, d), lambda b, s, j: (b, s, 0)),
            pl.BlockSpec((None, N_MOD, d), lambda b, s, j: (b, 0, 0)),
            pl.BlockSpec((None, 1, d), lambda b, s, j: (layer, 0, 0)),
            pl.BlockSpec((None, d, tc), lambda b, s, j: (j_layer, 0, j)),
            pl.BlockSpec((None, d, tc), lambda b, s, j: (j_layer, 0, nc + j)),
            pl.BlockSpec((None, d, tc), lambda b, s, j: (j_layer, 0, 2 * nc + j)),
            pl.BlockSpec((None, CONV_WIDTH, tc), lambda b, s, j: (j_layer, 0, j)),
            pl.BlockSpec((None, tc, d), lambda b, s, j: (j_layer, j, 0)),
        ],
        out_specs=pl.BlockSpec((None, ts, d), lambda b, s, j: (b, s, 0)),
        out_shape=jax.ShapeDtypeStruct(x.shape, F32),
        scratch_shapes=[
            pltpu.VMEM((ts, d), BF16),
            pltpu.VMEM((CONV_HALO + ts, tc), F32),
            pltpu.VMEM((nc, CONV_HALO, tc), F32),
        ],
        compiler_params=_params(3, est),
        name=f"conv_mixer_l{layer}",
    )(x, mods, gain, w_in, w_in, w_in, w_conv, w_out)


def _pool_kernel(x_ref, mod_ref, gain_ref, pw_ref, ps_ref, o_ref, carry_ref):
    s = pl.program_id(1)
    ts = x_ref.shape[0]

    @pl.when(s == 0)
    def _():
        carry_ref[...] = jnp.zeros_like(carry_ref)

    x = x_ref[...]
    h = _norm_mod(x, gain_ref[...], mod_ref[3:4, :], mod_ref[4:5, :])
    gate = mod_ref[5:6, :]
    scale = ps_ref[...]
    pos = s * ts + lax.broadcasted_iota(jnp.int32, (ts, 1), 0)
    for g, w in enumerate(POOL_WINDOWS):
        cols = slice(g * GROUP_DIM, (g + 1) * GROUP_DIM)
        hg = h[:, cols]
        acc = jnp.concatenate([carry_ref[:, cols], hg], axis=0)
        shift = 1
        while shift < w:
            acc = acc + pltpu.roll(acc, shift, axis=0)
            shift *= 2
        win_sum = acc[POOL_HALO:, :]
        inv_count = 1.0 / jnp.minimum(pos + 1, w).astype(F32)
        pooled = (win_sum * inv_count - hg).astype(BF16)
        y = jnp.dot(pooled, pw_ref[g], preferred_element_type=F32)
        o_ref[:, cols] = x[:, cols] + gate[:, cols] * (y * scale[:, cols])
    carry_ref[...] = h[ts - POOL_HALO :, :]


def _pool_mixer(x, mods, gain, pool_w, pool_scale, layer, j_layer):
    batch, seq, d = x.shape
    ts = POOL_TOKENS
    n_groups = len(POOL_WINDOWS)
    est = (4 * ts * d * 4) + 4 * ts * d * 4 + 2 * n_groups * GROUP_DIM * GROUP_DIM * 2
    return pl.pallas_call(
        _pool_kernel,
        grid=(batch, seq // ts),
        in_specs=[
            pl.BlockSpec((None, ts, d), lambda b, s: (b, s, 0)),
            pl.BlockSpec((None, N_MOD, d), lambda b, s: (b, 0, 0)),
            pl.BlockSpec((None, 1, d), lambda b, s: (layer, 0, 0)),
            pl.BlockSpec((None, n_groups, GROUP_DIM, GROUP_DIM), lambda b, s: (j_layer, 0, 0, 0)),
            pl.BlockSpec((None, 1, d), lambda b, s: (j_layer, 0, 0)),
        ],
        out_specs=pl.BlockSpec((None, ts, d), lambda b, s: (b, s, 0)),
        out_shape=jax.ShapeDtypeStruct(x.shape, F32),
        scratch_shapes=[pltpu.VMEM((POOL_HALO, d), F32)],
        compiler_params=_params(2, est),
        name=f"pool_mixer_l{layer}",
    )(x, mods, gain, pool_w, pool_scale)


def kernel(x, c, norm_ffn1, norm_mix, norm_ffn2, w_ada, b_ada, w_ffn1_in, w_ffn1_out,
           w_ffn2_in, w_ffn2_out, conv_in, conv_w, conv_out, pool_w, pool_scale, final_norm):
    batch, seq, d = x.shape
    assert (batch, seq, d) == (c.shape[0], seq, D_MODEL) and seq % FFN_TOKENS == 0

    mods_all = _ada_mods(c, w_ada, b_ada).reshape(DEPTH, batch, N_MOD, d)

    gain1 = norm_ffn1.reshape(DEPTH, 1, d)
    gain_mix = norm_mix.reshape(DEPTH, 1, d)
    gain2 = norm_ffn2.reshape(DEPTH, 1, d)
    w1_in, w1_out = w_ffn1_in.astype(BF16), w_ffn1_out.astype(BF16)
    w2_in, w2_out = w_ffn2_in.astype(BF16), w_ffn2_out.astype(BF16)
    cv_in, cv_out = conv_in.astype(BF16), conv_out.astype(BF16)
    p_w = pool_w.astype(BF16)
    p_scale = pool_scale.reshape(pool_scale.shape[0], 1, d)
    f_gain = final_norm.reshape(1, d)

    for i in range(DEPTH):
        mods = mods_all[i]
        x = _ffn(x, mods, gain1, w1_in, w1_out, i, 0)
        j = i // N_MIXERS
        if i % N_MIXERS == 0:
            x = _conv_mixer(x, mods, gain_mix, cv_in, conv_w, cv_out, i, j)
        else:
            x = _pool_mixer(x, mods, gain_mix, p_w, p_scale, i, j)
        x = _ffn(x, mods, gain2, w2_in, w2_out, i, 6, final_gain=f_gain if i == DEPTH - 1 else None)
    return x
```

```python
import functools

import jax
import jax.numpy as jnp
from jax import lax
from jax.experimental import pallas as pl
from jax.experimental.pallas import tpu as pltpu

D_MODEL = 2048
DEPTH = 4
N_MIXERS = 2
CONV_WIDTH = 3
POOL_WINDOWS = (2, 4, 8, 16)
GROUP_DIM = D_MODEL // len(POOL_WINDOWS)
D_FF = 5632
N_MOD = 9
EPS = 1e-6
FFN_RES_WEIGHT = 0.5

F32 = jnp.float32
BF16 = jnp.bfloat16

SUBLANES_F32 = 8
VMEM_BYTES_V7X = 64 * 1024 * 1024

FFN_TOKENS = 1024
FFN_HIDDEN = 512
CONV_TOKENS = 512
CONV_CHANNELS = 512
POOL_TOKENS = 512
ADA_COLS = 2048
ADA_ROWS = 16
POOL_HALO = 16
CONV_HALO = SUBLANES_F32


def _vmem_limit(estimate_bytes):
    return int(min(estimate_bytes * 3 // 2 + (4 << 20), VMEM_BYTES_V7X - (1 << 20)))


def _params(n_axes, estimate_bytes):
    return pltpu.CompilerParams(
        dimension_semantics=("arbitrary",) * n_axes,
        vmem_limit_bytes=_vmem_limit(estimate_bytes),
    )


def _rms(x, gain):
    ms = jnp.mean(x * x, axis=-1, keepdims=True)
    return (x * lax.rsqrt(ms + EPS)) * gain


def _norm_mod(x, gain, shift, scale):
    return _rms(x, gain) * (1.0 + scale) + shift


def _ada_kernel(c_ref, w_ref, b_ref, o_ref):
    c = c_ref[...]
    ca = (c * jax.nn.sigmoid(c)).astype(BF16)
    y = jnp.dot(ca, w_ref[...].astype(BF16), preferred_element_type=F32)
    o_ref[...] = y[: o_ref.shape[0], :] + b_ref[...]


def _ada_mods(c, w_ada, b_ada):
    batch = c.shape[0]
    n_out = w_ada.shape[-1]
    c_pad = jnp.pad(c, ((0, ADA_ROWS - batch), (0, 0)))
    b3 = b_ada.reshape(DEPTH, 1, n_out)
    est = 2 * D_MODEL * ADA_COLS * 4 + D_MODEL * ADA_COLS * 2
    return pl.pallas_call(
        _ada_kernel,
        grid=(DEPTH, n_out // ADA_COLS),
        in_specs=[
            pl.BlockSpec((ADA_ROWS, D_MODEL), lambda i, n: (0, 0)),
            pl.BlockSpec((None, D_MODEL, ADA_COLS), lambda i, n: (i, 0, n)),
            pl.BlockSpec((None, 1, ADA_COLS), lambda i, n: (i, 0, n)),
        ],
        out_specs=pl.BlockSpec((None, batch, ADA_COLS), lambda i, n: (i, 0, n)),
        out_shape=jax.ShapeDtypeStruct((DEPTH, batch, n_out), F32),
        compiler_params=_params(2, est),
        name="ada_mods",
    )(c_pad, w_ada, b3)


def _ffn_kernel(x_ref, mod_ref, gain_ref, wg_ref, wu_ref, wo_ref, *rest, mod_row, final):
    if final:
        fgain_ref, o_ref, h_ref = rest
    else:
        o_ref, h_ref = rest
    f = pl.program_id(2)

    @pl.when(f == 0)
    def _():
        shift = mod_ref[mod_row : mod_row + 1, :]
        scale = mod_ref[mod_row + 1 : mod_row + 2, :]
        h_ref[...] = _norm_mod(x_ref[...], gain_ref[...], shift, scale).astype(BF16)
        o_ref[...] = jnp.zeros_like(o_ref)

    h = h_ref[...]
    g = jnp.dot(h, wg_ref[...], preferred_element_type=F32)
    u = jnp.dot(h, wu_ref[...], preferred_element_type=F32)
    act = ((g * jax.nn.sigmoid(g)) * u).astype(BF16)
    o_ref[...] += jnp.dot(act, wo_ref[...], preferred_element_type=F32)

    @pl.when(f == pl.num_programs(2) - 1)
    def _():
        gate = mod_ref[mod_row + 2 : mod_row + 3, :]
        y = x_ref[...] + (FFN_RES_WEIGHT * gate) * o_ref[...]
        if final:
            y = _rms(y, fgain_ref[...])
        o_ref[...] = y


def _ffn(x, mods, gain, w_in, w_out, layer, mod_row, final_gain=None):
    batch, seq, d = x.shape
    ts, tf = FFN_TOKENS, FFN_HIDDEN
    nf = D_FF // tf
    final = final_gain is not None
    in_specs = [
        pl.BlockSpec((None, ts, d), lambda b, s, f: (b, s, 0)),
        pl.BlockSpec((None, N_MOD, d), lambda b, s, f: (b, 0, 0)),
        pl.BlockSpec((None, 1, d), lambda b, s, f: (layer, 0, 0)),
        pl.BlockSpec((None, d, tf), lambda b, s, f: (layer, 0, f)),
        pl.BlockSpec((None, d, tf), lambda b, s, f: (layer, 0, nf + f)),
        pl.BlockSpec((None, tf, d), lambda b, s, f: (layer, f, 0)),
    ]
    args = [x, mods, gain, w_in, w_in, w_out]
    if final:
        in_specs.append(pl.BlockSpec((1, d), lambda b, s, f: (0, 0)))
        args.append(final_gain)
    est = (4 * ts * d * 4) + ts * d * 2 + 2 * (3 * d * tf * 2) + 3 * ts * tf * 4
    return pl.pallas_call(
        functools.partial(_ffn_kernel, mod_row=mod_row, final=final),
        grid=(batch, seq // ts, nf),
        in_specs=in_specs,
        out_specs=pl.BlockSpec((None, ts, d), lambda b, s, f: (b, s, 0)),
        out_shape=jax.ShapeDtypeStruct(x.shape, F32),
        scratch_shapes=[pltpu.VMEM((ts, d), BF16)],
        compiler_params=_params(3, est),
        name=f"ffn_l{layer}_r{mod_row}",
    )(*args)


def _conv_kernel(x_ref, mod_ref, gain_ref, wb_ref, wc_ref, wv_ref, cw_ref, wo_ref,
                 o_ref, h_ref, u_ref, carry_ref):
    s = pl.program_id(1)
    j = pl.program_id(2)
    ts = x_ref.shape[0]

    @pl.when(j == 0)
    def _():
        shift = mod_ref[3:4, :]
        scale = mod_ref[4:5, :]
        h_ref[...] = _norm_mod(x_ref[...], gain_ref[...], shift, scale).astype(BF16)
        o_ref[...] = jnp.zeros_like(o_ref)

    @pl.when(s == 0)
    def _():
        carry_ref[j] = jnp.zeros(carry_ref.shape[1:], F32)

    h = h_ref[...]
    b_gate = jnp.dot(h, wb_ref[...], preferred_element_type=F32)
    c_gate = jnp.dot(h, wc_ref[...], preferred_element_type=F32)
    v = jnp.dot(h, wv_ref[...], preferred_element_type=F32)
    u = c_gate * v
    u_ref[0:CONV_HALO, :] = carry_ref[j]
    u_ref[CONV_HALO:, :] = u
    carry_ref[j] = u[ts - CONV_HALO :, :]
    y = cw_ref[CONV_WIDTH - 1 : CONV_WIDTH, :] * u
    for k in range(1, CONV_WIDTH):
        tap = cw_ref[CONV_WIDTH - 1 - k : CONV_WIDTH - k, :]
        y = y + tap * u_ref[CONV_HALO - k : CONV_HALO - k + ts, :]
    z = (b_gate * y).astype(BF16)
    o_ref[...] += jnp.dot(z, wo_ref[...], preferred_element_type=F32)

    @pl.when(j == pl.num_programs(2) - 1)
    def _():
        gate = mod_ref[5:6, :]
        o_ref[...] = x_ref[...] + gate * o_ref[...]


def _conv_mixer(x, mods, gain, w_in, w_conv, w_out, layer, j_layer):
    batch, seq, d = x.shape
    ts, tc = CONV_TOKENS, CONV_CHANNELS
    nc = d // tc
    est = (4 * ts * d * 4) + ts * d * 2 + 2 * (4 * d * tc * 2) + 5 * ts * tc * 4
    return pl.pallas_call(
        _conv_kernel,
        grid=(batch, seq // ts, nc),
        in_specs=[
            pl.BlockSpec((None, ts, d), lambda b, s, j: (b, s, 0)),
            pl.BlockSpec((None, N_MOD, d), lambda b, s, j: (b, 0, 0)),
            pl.BlockSpec((None, 1, d), lambda b, s, j: (layer, 0, 0)),
            pl.BlockSpec((None, d, tc), lambda b, s, j: (j_layer, 0, j)),
            pl.BlockSpec((None, d, tc), lambda b, s, j: (j_layer, 0, nc + j)),
            pl.BlockSpec((None, d, tc), lambda b, s, j: (j_layer, 0, 2 * nc + j)),
            pl.BlockSpec((None, CONV_WIDTH, tc), lambda b, s, j: (j_layer, 0, j)),
            pl.BlockSpec((None, tc, d), lambda b, s, j: (j_layer, j, 0)),
        ],
        out_specs=pl.BlockSpec((None, ts, d), lambda b, s, j: (b, s, 0)),
        out_shape=jax.ShapeDtypeStruct(x.shape, F32),
        scratch_shapes=[
            pltpu.VMEM((ts, d), BF16),
            pltpu.VMEM((CONV_HALO + ts, tc), F32),
            pltpu.VMEM((nc, CONV_HALO, tc), F32),
        ],
        compiler_params=_params(3, est),
        name=f"conv_mixer_l{layer}",
    )(x, mods, gain, w_in, w_in, w_in, w_conv, w_out)


def _pool_kernel(x_ref, mod_ref, gain_ref, pw_ref, ps_ref, o_ref, carry_ref):
    s = pl.program_id(1)
    ts = x_ref.shape[0]

    @pl.when(s == 0)
    def _():
        carry_ref[...] = jnp.zeros_like(carry_ref)

    x = x_ref[...]
    h = _norm_mod(x, gain_ref[...], mod_ref[3:4, :], mod_ref[4:5, :])
    gate = mod_ref[5:6, :]
    scale = ps_ref[...]
    pos = s * ts + lax.broadcasted_iota(jnp.int32, (ts, 1), 0)
    for g, w in enumerate(POOL_WINDOWS):
        cols = slice(g * GROUP_DIM, (g + 1) * GROUP_DIM)
        hg = h[:, cols]
        acc = jnp.concatenate([carry_ref[:, cols], hg], axis=0)
        shift = 1
        while shift < w:
            acc = acc + pltpu.roll(acc, shift, axis=0)
            shift *= 2
        win_sum = acc[POOL_HALO:, :]
        inv_count = 1.0 / jnp.minimum(pos + 1, w).astype(F32)
        pooled = (win_sum * inv_count - hg).astype(BF16)
        y = jnp.dot(pooled, pw_ref[g], preferred_element_type=F32)
        o_ref[:, cols] = x[:, cols] + gate[:, cols] * (y * scale[:, cols])
    carry_ref[...] = h[ts - POOL_HALO :, :]


def _pool_mixer(x, mods, gain, pool_w, pool_scale, layer, j_layer):
    batch, seq, d = x.shape
    ts = POOL_TOKENS
    n_groups = len(POOL_WINDOWS)
    est = (4 * ts * d * 4) + 4 * ts * d * 4 + 2 * n_groups * GROUP_DIM * GROUP_DIM * 2
    return pl.pallas_call(
        _pool_kernel,
        grid=(batch, seq // ts),
        in_specs=[
            pl.BlockSpec((None, ts, d), lambda b, s: (b, s, 0)),
            pl.BlockSpec((None, N_MOD, d), lambda b, s: (b, 0, 0)),
            pl.BlockSpec((None, 1, d), lambda b, s: (layer, 0, 0)),
            pl.BlockSpec((None, n_groups, GROUP_DIM, GROUP_DIM), lambda b, s: (j_layer, 0, 0, 0)),
            pl.BlockSpec((None, 1, d), lambda b, s: (j_layer, 0, 0)),
        ],
        out_specs=pl.BlockSpec((None, ts, d), lambda b, s: (b, s, 0)),
        out_shape=jax.ShapeDtypeStruct(x.shape, F32),
        scratch_shapes=[pltpu.VMEM((POOL_HALO, d), F32)],
        compiler_params=_params(2, est),
        name=f"pool_mixer_l{layer}",
    )(x, mods, gain, pool_w, pool_scale)


def kernel(x, c, norm_ffn1, norm_mix, norm_ffn2, w_ada, b_ada, w_ffn1_in, w_ffn1_out,
           w_ffn2_in, w_ffn2_out, conv_in, conv_w, conv_out, pool_w, pool_scale, final_norm):
    batch, seq, d = x.shape
    assert (batch, seq, d) == (c.shape[0], seq, D_MODEL) and seq % FFN_TOKENS == 0

    mods_all = _ada_mods(c, w_ada, b_ada).reshape(DEPTH, batch, N_MOD, d)

    gain1 = norm_ffn1.reshape(DEPTH, 1, d)
    gain_mix = norm_mix.reshape(DEPTH, 1, d)
    gain2 = norm_ffn2.reshape(DEPTH, 1, d)
    w1_in, w1_out = w_ffn1_in.astype(BF16), w_ffn1_out.astype(BF16)
    w2_in, w2_out = w_ffn2_in.astype(BF16), w_ffn2_out.astype(BF16)
    cv_in, cv_out = conv_in.astype(BF16), conv_out.astype(BF16)
    p_w = pool_w.astype(BF16)
    p_scale = pool_scale.reshape(pool_scale.shape[0], 1, d)
    f_gain = final_norm.reshape(1, d)

    for i in range(DEPTH):
        mods = mods_all[i]
        x = _ffn(x, mods, gain1, w1_in, w1_out, i, 0)
        j = i // N_MIXERS
        if i % N_MIXERS == 0:
            x = _conv_mixer(x, mods, gain_mix, cv_in, conv_w, cv_out, i, j)
        else:
            x = _pool_mixer(x, mods, gain_mix, p_w, p_scale, i, j)
        x = _ffn(x, mods, gain2, w2_in, w2_out, i, 6, final_gain=f_gain if i == DEPTH - 1 else None)
    return x
```

```python
import functools

import jax
import jax.numpy as jnp
from jax import lax
from jax.experimental import pallas as pl
from jax.experimental.pallas import tpu as pltpu

D_MODEL = 2048
DEPTH = 4
N_MIXERS = 2
CONV_WIDTH = 3
POOL_WINDOWS = (2, 4, 8, 16)
GROUP_DIM = D_MODEL // len(POOL_WINDOWS)
D_FF = 5632
N_MOD = 9
EPS = 1e-6
FFN_RES_WEIGHT = 0.5

F32 = jnp.float32
BF16 = jnp.bfloat16

SUBLANES_F32 = 8
SUBLANES_BF16 = 16
VMEM_BYTES_V7X = 64 * 1024 * 1024

FFN_TOKENS = 1024
FFN_HIDDEN = 512
FFN_ROW_CHUNK = 256
CONV_TOKENS = 1024
CONV_CHANNELS = 512
CONV_ROW_CHUNK = 256
POOL_TOKENS = 512
ADA_COLS = 2048
ADA_ROWS = 16
POOL_HALO = 16
CONV_HALO = SUBLANES_F32


def _vmem_limit(estimate_bytes):
    return int(min(estimate_bytes * 3 // 2 + (4 << 20), VMEM_BYTES_V7X - (1 << 20)))


def _params(n_axes, estimate_bytes):
    return pltpu.CompilerParams(
        dimension_semantics=("arbitrary",) * n_axes,
        vmem_limit_bytes=_vmem_limit(estimate_bytes),
    )


def _rms(x, gain):
    ms = jnp.mean(x * x, axis=-1, keepdims=True)
    return (x * lax.rsqrt(ms + EPS)) * gain


def _norm_mod(x, gain, shift, scale):
    return _rms(x, gain) * (1.0 + scale) + shift


def _ada_kernel(c_ref, w_ref, b_ref, o_ref):
    c = c_ref[...]
    ca = (c * jax.nn.sigmoid(c)).astype(BF16)
    y = jnp.dot(ca, w_ref[...].astype(BF16), preferred_element_type=F32)
    o_ref[...] = y[: o_ref.shape[0], :] + b_ref[...]


def _ada_mods(c, w_ada, b_ada):
    batch = c.shape[0]
    n_out = w_ada.shape[-1]
    c_pad = jnp.pad(c, ((0, ADA_ROWS - batch), (0, 0)))
    b3 = b_ada.reshape(DEPTH, 1, n_out)
    est = 2 * D_MODEL * ADA_COLS * 4 + D_MODEL * ADA_COLS * 2
    return pl.pallas_call(
        _ada_kernel,
        grid=(DEPTH, n_out // ADA_COLS),
        in_specs=[
            pl.BlockSpec((ADA_ROWS, D_MODEL), lambda i, n: (0, 0)),
            pl.BlockSpec((None, D_MODEL, ADA_COLS), lambda i, n: (i, 0, n)),
            pl.BlockSpec((None, 1, ADA_COLS), lambda i, n: (i, 0, n)),
        ],
        out_specs=pl.BlockSpec((None, batch, ADA_COLS), lambda i, n: (i, 0, n)),
        out_shape=jax.ShapeDtypeStruct((DEPTH, batch, n_out), F32),
        compiler_params=_params(2, est),
        name="ada_mods",
    )(c_pad, w_ada, b3)


def _ffn_kernel(x_ref, mod_ref, gain_ref, wg_ref, wu_ref, wo_ref, *rest, mod_row, final, n_cast):
    if final:
        fgain_ref, rest = rest[0], rest[1:]
    cast_in = rest[:n_cast]
    o_ref = rest[n_cast]
    cast_out = rest[n_cast + 1 : 2 * n_cast + 1]
    h_ref = rest[2 * n_cast + 1]
    f = pl.program_id(2)
    ts = x_ref.shape[0]

    def cast_weights():
        for src, dst in zip(cast_in, cast_out):
            dst[...] = src[...].astype(BF16)

    def swiglu_partial(h):
        g = jnp.dot(h, wg_ref[...], preferred_element_type=F32)
        u = jnp.dot(h, wu_ref[...], preferred_element_type=F32)
        act = ((g * jax.nn.sigmoid(g)) * u).astype(BF16)
        return jnp.dot(act, wo_ref[...], preferred_element_type=F32)

    @pl.when(f == 0)
    def _():
        shift = mod_ref[mod_row : mod_row + 1, :]
        scale = mod_ref[mod_row + 1 : mod_row + 2, :]
        gain = gain_ref[...]
        cast_weights()
        for r in range(ts // FFN_ROW_CHUNK):
            rows = pl.ds(r * FFN_ROW_CHUNK, FFN_ROW_CHUNK)
            h = _norm_mod(x_ref[rows, :], gain, shift, scale).astype(BF16)
            h_ref[rows, :] = h
            o_ref[rows, :] = swiglu_partial(h)

    last = pl.num_programs(2) - 1

    @pl.when(jnp.logical_and(f > 0, f < last))
    def _():
        cast_weights()
        o_ref[...] += swiglu_partial(h_ref[...])

    @pl.when(f == last)
    def _():
        gate = FFN_RES_WEIGHT * mod_ref[mod_row + 2 : mod_row + 3, :]
        cast_weights()
        for r in range(ts // FFN_ROW_CHUNK):
            rows = pl.ds(r * FFN_ROW_CHUNK, FFN_ROW_CHUNK)
            y = x_ref[rows, :] + gate * (o_ref[rows, :] + swiglu_partial(h_ref[rows, :]))
            if final:
                y = _rms(y, fgain_ref[...])
            o_ref[rows, :] = y


def _cast_block_rows(n_rows, n_steps):
    rows = SUBLANES_BF16
    while n_rows % rows or n_rows // rows > n_steps:
        rows += SUBLANES_BF16
    return rows


def _ffn(x, mods, gain, w_in, w_out, layer, mod_row, final_gain=None, cast=()):
    batch, seq, d = x.shape
    ts, tf = FFN_TOKENS, FFN_HIDDEN
    nf = D_FF // tf
    ns = seq // ts
    n_steps = batch * ns * nf
    final = final_gain is not None
    in_specs = [
        pl.BlockSpec((None, ts, d), lambda b, s, f: (b, s, 0)),
        pl.BlockSpec((None, N_MOD, d), lambda b, s, f: (b, 0, 0)),
        pl.BlockSpec((None, 1, d), lambda b, s, f: (layer, 0, 0)),
        pl.BlockSpec((d, tf), lambda b, s, f: (0, f)),
        pl.BlockSpec((d, tf), lambda b, s, f: (0, nf + f)),
        pl.BlockSpec((tf, d), lambda b, s, f: (f, 0)),
    ]
    args = [x, mods, gain, w_in, w_in, w_out]
    if final:
        in_specs.append(pl.BlockSpec((1, d), lambda b, s, f: (0, 0)))
        args.append(final_gain)
    out_specs = [pl.BlockSpec((None, ts, d), lambda b, s, f: (b, s, 0))]
    out_shape = [jax.ShapeDtypeStruct(x.shape, F32)]
    est = (4 * ts * d * 4) + ts * d * 2 + 2 * (3 * d * tf * 2) + 3 * ts * tf * 4
    for w, w_layer in cast:
        _, n_rows, n_cols = w.shape
        rows = _cast_block_rows(n_rows, n_steps)
        n_blocks = n_rows // rows

        def block_of(b, s, f, n_blocks=n_blocks):
            return jnp.minimum((b * ns + s) * nf + f, n_blocks - 1)

        in_specs.append(pl.BlockSpec(
            (None, rows, n_cols), lambda b, s, f, w_layer=w_layer, blk=block_of: (w_layer, blk(b, s, f), 0)))
        args.append(w)
        out_specs.append(pl.BlockSpec((rows, n_cols), lambda b, s, f, blk=block_of: (blk(b, s, f), 0)))
        out_shape.append(jax.ShapeDtypeStruct((n_rows, n_cols), BF16))
        est += 2 * rows * n_cols * (4 + 2)
    outs = pl.pallas_call(
        functools.partial(_ffn_kernel, mod_row=mod_row, final=final, n_cast=len(cast)),
        grid=(batch, ns, nf),
        in_specs=in_specs,
        out_specs=out_specs,
        out_shape=out_shape,
        scratch_shapes=[pltpu.VMEM((ts, d), BF16)],
        compiler_params=_params(3, est),
        name=f"ffn_l{layer}_r{mod_row}",
    )(*args)
    return outs[0], list(outs[1:])


def _conv_kernel(x_ref, mod_ref, gain_ref, wb_ref, wc_ref, wv_ref, cw_ref, wo_ref,
                 o_ref, h_ref, u_ref, carry_ref):
    s = pl.program_id(1)
    j = pl.program_id(2)
    last = pl.num_programs(2) - 1
    ts = x_ref.shape[0]

    @pl.when(s == 0)
    def _():
        carry_ref[j] = jnp.zeros(carry_ref.shape[1:], F32)

    def mixer_partial(h, row0, n_rows):
        b_gate = jnp.dot(h, wb_ref[...], preferred_element_type=F32)
        c_gate = jnp.dot(h, wc_ref[...], preferred_element_type=F32)
        v = jnp.dot(h, wv_ref[...], preferred_element_type=F32)
        u = c_gate * v
        u_ref[pl.ds(CONV_HALO + row0, n_rows), :] = u
        y = cw_ref[CONV_WIDTH - 1 : CONV_WIDTH, :] * u
        for k in range(1, CONV_WIDTH):
            tap = cw_ref[CONV_WIDTH - 1 - k : CONV_WIDTH - k, :]
            y = y + tap * u_ref[pl.ds(CONV_HALO + row0 - k, n_rows), :]
        z = (b_gate * y).astype(BF16)
        return jnp.dot(z, wo_ref[...], preferred_element_type=F32)

    u_ref[0:CONV_HALO, :] = carry_ref[j]
    row_chunks = [(r * CONV_ROW_CHUNK, CONV_ROW_CHUNK) for r in range(ts // CONV_ROW_CHUNK)]

    @pl.when(j == 0)
    def _():
        shift = mod_ref[3:4, :]
        scale = mod_ref[4:5, :]
        gain = gain_ref[...]
        for row0, n in row_chunks:
            rows = pl.ds(row0, n)
            h = _norm_mod(x_ref[rows, :], gain, shift, scale).astype(BF16)
            h_ref[rows, :] = h
            o_ref[rows, :] = mixer_partial(h, row0, n)

    @pl.when(jnp.logical_and(j > 0, j < last))
    def _():
        o_ref[...] += mixer_partial(h_ref[...], 0, ts)

    @pl.when(j == last)
    def _():
        gate = mod_ref[5:6, :]
        for row0, n in row_chunks:
            rows = pl.ds(row0, n)
            part = mixer_partial(h_ref[rows, :], row0, n)
            o_ref[rows, :] = x_ref[rows, :] + gate * (o_ref[rows, :] + part)

    carry_ref[j] = u_ref[pl.ds(ts, CONV_HALO), :]


def _conv_mixer(x, mods, gain, w_in, w_conv, w_out, layer, j_layer):
    batch, seq, d = x.shape
    ts, tc = CONV_TOKENS, CONV_CHANNELS
    nc = d // tc
    est = (4 * ts * d * 4) + ts * d * 2 + 2 * (4 * d * tc * 2) + 5 * ts * tc * 4
    return pl.pallas_call(
        _conv_kernel,
        grid=(batch, seq // ts, nc),
        in_specs=[
            pl.BlockSpec((None, ts, d), lambda b, s, j: (b, s, 0)),
            pl.BlockSpec((None, N_MOD, d), lambda b, s, j: (b, 0, 0)),
            pl.BlockSpec((None, 1, d), lambda b, s, j: (layer, 0, 0)),
            pl.BlockSpec((d, tc), lambda b, s, j: (0, j)),
            pl.BlockSpec((d, tc), lambda b, s, j: (0, nc + j)),
            pl.BlockSpec((d, tc), lambda b, s, j: (0, 2 * nc + j)),
            pl.BlockSpec((None, CONV_WIDTH, tc), lambda b, s, j: (j_layer, 0, j)),
            pl.BlockSpec((tc, d), lambda b, s, j: (j, 0)),
        ],
        out_specs=pl.BlockSpec((None, ts, d), lambda b, s, j: (b, s, 0)),
        out_shape=jax.ShapeDtypeStruct(x.shape, F32),
        scratch_shapes=[
            pltpu.VMEM((ts, d), BF16),
            pltpu.VMEM((CONV_HALO + ts, tc), F32),
            pltpu.VMEM((nc, CONV_HALO, tc), F32),
        ],
        compiler_params=_params(3, est),
        name=f"conv_mixer_l{layer}",
    )(x, mods, gain, w_in, w_in, w_in, w_conv, w_out)


def _pool_kernel(x_ref, mod_ref, gain_ref, pw_ref, ps_ref, o_ref, carry_ref):
    s = pl.program_id(1)
    ts = x_ref.shape[0]

    @pl.when(s == 0)
    def _():
        carry_ref[...] = jnp.zeros_like(carry_ref)

    x = x_ref[...]
    h = _norm_mod(x, gain_ref[...], mod_ref[3:4, :], mod_ref[4:5, :])
    gate = mod_ref[5:6, :]
    scale = ps_ref[...]
    pos = s * ts + lax.broadcasted_iota(jnp.int32, (ts, 1), 0)
    for g, w in enumerate(POOL_WINDOWS):
        cols = slice(g * GROUP_DIM, (g + 1) * GROUP_DIM)
        hg = h[:, cols]
        acc = jnp.concatenate([carry_ref[:, cols], hg], axis=0)
        shift = 1
        while shift < w:
            acc = acc + pltpu.roll(acc, shift, axis=0)
            shift *= 2
        win_sum = acc[POOL_HALO:, :]
        inv_count = 1.0 / jnp.minimum(pos + 1, w).astype(F32)
        pooled = (win_sum * inv_count - hg).astype(BF16)
        y = jnp.dot(pooled, pw_ref[g], preferred_element_type=F32)
        o_ref[:, cols] = x[:, cols] + gate[:, cols] * (y * scale[:, cols])
    carry_ref[...] = h[ts - POOL_HALO :, :]


def _pool_mixer(x, mods, gain, pool_w, pool_scale, layer, j_layer):
    batch, seq, d = x.shape
    ts = POOL_TOKENS
    n_groups = len(POOL_WINDOWS)
    est = (4 * ts * d * 4) + 4 * ts * d * 4 + 2 * n_groups * GROUP_DIM * GROUP_DIM * 2
    return pl.pallas_call(
        _pool_kernel,
        grid=(batch, seq // ts),
        in_specs=[
            pl.BlockSpec((None, ts, d), lambda b, s: (b, s, 0)),
            pl.BlockSpec((None, N_MOD, d), lambda b, s: (b, 0, 0)),
            pl.BlockSpec((None, 1, d), lambda b, s: (layer, 0, 0)),
            pl.BlockSpec((None, n_groups, GROUP_DIM, GROUP_DIM), lambda b, s: (j_layer, 0, 0, 0)),
            pl.BlockSpec((None, 1, d), lambda b, s: (j_layer, 0, 0)),
        ],
        out_specs=pl.BlockSpec((None, ts, d), lambda b, s: (b, s, 0)),
        out_shape=jax.ShapeDtypeStruct(x.shape, F32),
        scratch_shapes=[pltpu.VMEM((POOL_HALO, d), F32)],
        compiler_params=_params(2, est),
        name=f"pool_mixer_l{layer}",
    )(x, mods, gain, pool_w, pool_scale)


def kernel(x, c, norm_ffn1, norm_mix, norm_ffn2, w_ada, b_ada, w_ffn1_in, w_ffn1_out,
           w_ffn2_in, w_ffn2_out, conv_in, conv_w, conv_out, pool_w, pool_scale, final_norm):
    batch, seq, d = x.shape
    assert (batch, seq, d) == (c.shape[0], seq, D_MODEL) and seq % FFN_TOKENS == 0

    mods_all = _ada_mods(c, w_ada, b_ada).reshape(DEPTH, batch, N_MOD, d)

    gain1 = norm_ffn1.reshape(DEPTH, 1, d)
    gain_mix = norm_mix.reshape(DEPTH, 1, d)
    gain2 = norm_ffn2.reshape(DEPTH, 1, d)
    p_w = pool_w.astype(BF16)
    p_scale = pool_scale.reshape(pool_scale.shape[0], 1, d)
    f_gain = final_norm.reshape(1, d)

    w_in, w_out = w_ffn1_in[0].astype(BF16), w_ffn1_out[0].astype(BF16)
    for i in range(DEPTH):
        mods = mods_all[i]
        j = i // N_MIXERS
        is_conv = i % N_MIXERS == 0
        cast = [(w_ffn2_in, i), (w_ffn2_out, i)]
        if is_conv:
            cast += [(conv_in, j), (conv_out, j)]
        x, cast_out = _ffn(x, mods, gain1, w_in, w_out, i, 0, cast=cast)
        w_in, w_out = cast_out[:2]
        if is_conv:
            x = _conv_mixer(x, mods, gain_mix, cast_out[2], conv_w, cast_out[3], i, j)
        else:
            x = _pool_mixer(x, mods, gain_mix, p_w, p_scale, i, j)
        if i == DEPTH - 1:
            x, _ = _ffn(x, mods, gain2, w_in, w_out, i, 6, final_gain=f_gain)
        else:
            cast = [(w_ffn1_in, i + 1), (w_ffn1_out, i + 1)]
            x, (w_in, w_out) = _ffn(x, mods, gain2, w_in, w_out, i, 6, cast=cast)
    return x
```

```python
import functools

import jax
import jax.numpy as jnp
from jax import lax
from jax.experimental import pallas as pl
from jax.experimental.pallas import tpu as pltpu

D_MODEL = 2048
DEPTH = 4
N_MIXERS = 2
CONV_WIDTH = 3
POOL_WINDOWS = (2, 4, 8, 16)
GROUP_DIM = D_MODEL // len(POOL_WINDOWS)
D_FF = 5632
N_MOD = 9
EPS = 1e-6
FFN_RES_WEIGHT = 0.5

F32 = jnp.float32
BF16 = jnp.bfloat16

SUBLANES_F32 = 8
SUBLANES_BF16 = 16
VMEM_BYTES_V7X = 64 * 1024 * 1024

FFN_TOKENS = 1024
FFN_HIDDEN = 512
FFN_ROW_CHUNK = 256
CONV_TOKENS = 1024
CONV_CHANNELS = 512
CONV_ROW_CHUNK = 256
POOL_TOKENS = 512
ADA_COLS = 2048
ADA_ROWS = 16
POOL_HALO = 16
CONV_HALO = SUBLANES_F32


def _vmem_limit(estimate_bytes):
    return int(min(estimate_bytes * 3 // 2 + (4 << 20), VMEM_BYTES_V7X - (1 << 20)))


def _params(n_axes, estimate_bytes):
    return pltpu.CompilerParams(
        dimension_semantics=("arbitrary",) * n_axes,
        vmem_limit_bytes=_vmem_limit(estimate_bytes),
    )


def _rms(x, gain):
    ms = jnp.mean(x * x, axis=-1, keepdims=True)
    return (x * lax.rsqrt(ms + EPS)) * gain


def _norm_mod(x, gain, shift, scale):
    return _rms(x, gain) * (1.0 + scale) + shift


def _ada_kernel(c_ref, w_ref, b_ref, o_ref):
    c = c_ref[...]
    ca = (c * jax.nn.sigmoid(c)).astype(BF16)
    y = jnp.dot(ca, w_ref[...].astype(BF16), preferred_element_type=F32)
    o_ref[...] = y[: o_ref.shape[0], :] + b_ref[...]


def _ada_mods(c, w_ada, b_ada):
    batch = c.shape[0]
    n_out = w_ada.shape[-1]
    c_pad = jnp.pad(c, ((0, ADA_ROWS - batch), (0, 0)))
    b3 = b_ada.reshape(DEPTH, 1, n_out)
    est = 2 * D_MODEL * ADA_COLS * 4 + D_MODEL * ADA_COLS * 2
    return pl.pallas_call(
        _ada_kernel,
        grid=(DEPTH, n_out // ADA_COLS),
        in_specs=[
            pl.BlockSpec((ADA_ROWS, D_MODEL), lambda i, n: (0, 0)),
            pl.BlockSpec((None, D_MODEL, ADA_COLS), lambda i, n: (i, 0, n)),
            pl.BlockSpec((None, 1, ADA_COLS), lambda i, n: (i, 0, n)),
        ],
        out_specs=pl.BlockSpec((None, batch, ADA_COLS), lambda i, n: (i, 0, n)),
        out_shape=jax.ShapeDtypeStruct((DEPTH, batch, n_out), F32),
        compiler_params=_params(2, est),
        name="ada_mods",
    )(c_pad, w_ada, b3)


def _ffn_kernel(x_ref, mod_ref, gain_ref, wg_ref, wu_ref, wo_ref, *rest, mod_row, final, n_cast,
                row_chunk):
    if final:
        fgain_ref, rest = rest[0], rest[1:]
    cast_in = rest[:n_cast]
    o_ref = rest[n_cast]
    cast_out = rest[n_cast + 1 : 2 * n_cast + 1]
    h_ref = rest[2 * n_cast + 1]
    f = pl.program_id(2)
    ts = x_ref.shape[0]

    def cast_weights():
        for src, dst in zip(cast_in, cast_out):
            dst[...] = src[...].astype(BF16)

    def swiglu_partial(h):
        g = jnp.dot(h, wg_ref[...], preferred_element_type=F32)
        u = jnp.dot(h, wu_ref[...], preferred_element_type=F32)
        act = ((g * jax.nn.sigmoid(g)) * u).astype(BF16)
        return jnp.dot(act, wo_ref[...], preferred_element_type=F32)

    @pl.when(f == 0)
    def _():
        shift = mod_ref[mod_row : mod_row + 1, :]
        scale = mod_ref[mod_row + 1 : mod_row + 2, :]
        gain = gain_ref[...]
        cast_weights()
        for r in range(ts // row_chunk):
            rows = pl.ds(r * row_chunk, row_chunk)
            h = _norm_mod(x_ref[rows, :], gain, shift, scale).astype(BF16)
            h_ref[rows, :] = h
            o_ref[rows, :] = swiglu_partial(h)

    last = pl.num_programs(2) - 1

    @pl.when(jnp.logical_and(f > 0, f < last))
    def _():
        cast_weights()
        o_ref[...] += swiglu_partial(h_ref[...])

    @pl.when(f == last)
    def _():
        gate = FFN_RES_WEIGHT * mod_ref[mod_row + 2 : mod_row + 3, :]
        cast_weights()
        for r in range(ts // row_chunk):
            rows = pl.ds(r * row_chunk, row_chunk)
            y = x_ref[rows, :] + gate * (o_ref[rows, :] + swiglu_partial(h_ref[rows, :]))
            if final:
                y = _rms(y, fgain_ref[...])
            o_ref[rows, :] = y


def _cast_block_rows(n_rows, n_steps):
    rows = SUBLANES_BF16
    while n_rows % rows or n_rows // rows > n_steps:
        rows += SUBLANES_BF16
    return rows


def _ffn(x, mods, gain, w_in, w_out, layer, mod_row, final_gain=None, cast=()):
    batch, seq, d = x.shape
    ts, tf = FFN_TOKENS, FFN_HIDDEN
    nf = D_FF // tf
    ns = seq // ts
    n_steps = batch * ns * nf
    final = final_gain is not None
    in_specs = [
        pl.BlockSpec((None, ts, d), lambda b, s, f: (b, s, 0)),
        pl.BlockSpec((None, N_MOD, d), lambda b, s, f: (b, 0, 0)),
        pl.BlockSpec((None, 1, d), lambda b, s, f: (layer, 0, 0)),
        pl.BlockSpec((d, tf), lambda b, s, f: (0, f)),
        pl.BlockSpec((d, tf), lambda b, s, f: (0, nf + f)),
        pl.BlockSpec((tf, d), lambda b, s, f: (f, 0)),
    ]
    args = [x, mods, gain, w_in, w_in, w_out]
    if final:
        in_specs.append(pl.BlockSpec((1, d), lambda b, s, f: (0, 0)))
        args.append(final_gain)
    out_specs = [pl.BlockSpec((None, ts, d), lambda b, s, f: (b, s, 0))]
    out_shape = [jax.ShapeDtypeStruct(x.shape, F32)]
    est = (4 * ts * d * 4) + ts * d * 2 + 2 * (3 * d * tf * 2) + 3 * ts * tf * 4
    for w, w_layer in cast:
        _, n_rows, n_cols = w.shape
        rows = _cast_block_rows(n_rows, n_steps)
        n_blocks = n_rows // rows

        def block_of(b, s, f, n_blocks=n_blocks):
            return jnp.minimum((b * ns + s) * nf + f, n_blocks - 1)

        in_specs.append(pl.BlockSpec(
            (None, rows, n_cols), lambda b, s, f, w_layer=w_layer, blk=block_of: (w_layer, blk(b, s, f), 0)))
        args.append(w)
        out_specs.append(pl.BlockSpec((rows, n_cols), lambda b, s, f, blk=block_of: (blk(b, s, f), 0)))
        out_shape.append(jax.ShapeDtypeStruct((n_rows, n_cols), BF16))
        est += 2 * rows * n_cols * (4 + 2)
    outs = pl.pallas_call(
        functools.partial(_ffn_kernel, mod_row=mod_row, final=final, n_cast=len(cast),
                          row_chunk=FFN_ROW_CHUNK if mod_row == 0 else 2 * FFN_ROW_CHUNK),
        grid=(batch, ns, nf),
        in_specs=in_specs,
        out_specs=out_specs,
        out_shape=out_shape,
        scratch_shapes=[pltpu.VMEM((ts, d), BF16)],
        compiler_params=_params(3, est),
        name=f"ffn_l{layer}_r{mod_row}",
    )(*args)
    return outs[0], list(outs[1:])


def _conv_kernel(x_ref, mod_ref, gain_ref, wb_ref, wc_ref, wv_ref, cw_ref, wo_ref,
                 o_ref, h_ref, u_ref, carry_ref, *, row_chunk):
    s = pl.program_id(1)
    j = pl.program_id(2)
    last = pl.num_programs(2) - 1
    ts = x_ref.shape[0]

    @pl.when(s == 0)
    def _():
        carry_ref[j] = jnp.zeros(carry_ref.shape[1:], F32)

    def mixer_partial(h, row0, n_rows):
        b_gate = jnp.dot(h, wb_ref[...], preferred_element_type=F32)
        c_gate = jnp.dot(h, wc_ref[...], preferred_element_type=F32)
        v = jnp.dot(h, wv_ref[...], preferred_element_type=F32)
        u = c_gate * v
        u_ref[pl.ds(CONV_HALO + row0, n_rows), :] = u
        y = cw_ref[CONV_WIDTH - 1 : CONV_WIDTH, :] * u
        for k in range(1, CONV_WIDTH):
            tap = cw_ref[CONV_WIDTH - 1 - k : CONV_WIDTH - k, :]
            y = y + tap * u_ref[pl.ds(CONV_HALO + row0 - k, n_rows), :]
        z = (b_gate * y).astype(BF16)
        return jnp.dot(z, wo_ref[...], preferred_element_type=F32)

    u_ref[0:CONV_HALO, :] = carry_ref[j]
    row_chunks = [(r * row_chunk, row_chunk) for r in range(ts // row_chunk)]

    @pl.when(j == 0)
    def _():
        shift = mod_ref[3:4, :]
        scale = mod_ref[4:5, :]
        gain = gain_ref[...]
        for row0, n in row_chunks:
            rows = pl.ds(row0, n)
            h = _norm_mod(x_ref[rows, :], gain, shift, scale).astype(BF16)
            h_ref[rows, :] = h
            o_ref[rows, :] = mixer_partial(h, row0, n)

    @pl.when(jnp.logical_and(j > 0, j < last))
    def _():
        o_ref[...] += mixer_partial(h_ref[...], 0, ts)

    @pl.when(j == last)
    def _():
        gate = mod_ref[5:6, :]
        for row0, n in row_chunks:
            rows = pl.ds(row0, n)
            part = mixer_partial(h_ref[rows, :], row0, n)
            o_ref[rows, :] = x_ref[rows, :] + gate * (o_ref[rows, :] + part)

    carry_ref[j] = u_ref[pl.ds(ts, CONV_HALO), :]


def _conv_mixer(x, mods, gain, w_in, w_conv, w_out, layer, j_layer):
    batch, seq, d = x.shape
    ts, tc = CONV_TOKENS, CONV_CHANNELS
    nc = d // tc
    est = (4 * ts * d * 4) + ts * d * 2 + 2 * (4 * d * tc * 2) + 5 * ts * tc * 4
    return pl.pallas_call(
        functools.partial(_conv_kernel,
                          row_chunk=CONV_ROW_CHUNK if layer == 0 else 2 * CONV_ROW_CHUNK),
        grid=(batch, seq // ts, nc),
        in_specs=[
            pl.BlockSpec((None, ts, d), lambda b, s, j: (b, s, 0)),
            pl.BlockSpec((None, N_MOD, d), lambda b, s, j: (b, 0, 0)),
            pl.BlockSpec((None, 1, d), lambda b, s, j: (layer, 0, 0)),
            pl.BlockSpec((d, tc), lambda b, s, j: (0, j)),
            pl.BlockSpec((d, tc), lambda b, s, j: (0, nc + j)),
            pl.BlockSpec((d, tc), lambda b, s, j: (0, 2 * nc + j)),
            pl.BlockSpec((None, CONV_WIDTH, tc), lambda b, s, j: (j_layer, 0, j)),
            pl.BlockSpec((tc, d), lambda b, s, j: (j, 0)),
        ],
        out_specs=pl.BlockSpec((None, ts, d), lambda b, s, j: (b, s, 0)),
        out_shape=jax.ShapeDtypeStruct(x.shape, F32),
        scratch_shapes=[
            pltpu.VMEM((ts, d), BF16),
            pltpu.VMEM((CONV_HALO + ts, tc), F32),
            pltpu.VMEM((nc, CONV_HALO, tc), F32),
        ],
        compiler_params=_params(3, est),
        name=f"conv_mixer_l{layer}",
    )(x, mods, gain, w_in, w_in, w_in, w_conv, w_out)


def _pool_mix(x, pos0, carry_ref, mod_ref, gain, pw_ref, scale, store):
    n = x.shape[0]
    h = _norm_mod(x, gain, mod_ref[3:4, :], mod_ref[4:5, :])
    gate = mod_ref[5:6, :]
    pos = pos0 + lax.broadcasted_iota(jnp.int32, (n, 1), 0)
    for g, w in enumerate(POOL_WINDOWS):
        cols = slice(g * GROUP_DIM, (g + 1) * GROUP_DIM)
        hg = h[:, cols]
        acc = jnp.concatenate([carry_ref[:, cols], hg], axis=0)
        shift = 1
        while shift < w:
            acc = acc + pltpu.roll(acc, shift, axis=0)
            shift *= 2
        win_sum = acc[POOL_HALO:, :]
        inv_count = 1.0 / jnp.minimum(pos + 1, w).astype(F32)
        pooled = (win_sum * inv_count - hg).astype(BF16)
        y = jnp.dot(pooled, pw_ref[g], preferred_element_type=F32)
        store(cols, x[:, cols] + gate[:, cols] * (y * scale[:, cols]))
    carry_ref[...] = h[n - POOL_HALO :, :]


def _pool_kernel(x_ref, mod_ref, gain_ref, pw_ref, ps_ref, o_ref, carry_ref):
    s = pl.program_id(1)

    @pl.when(s == 0)
    def _():
        carry_ref[...] = jnp.zeros_like(carry_ref)

    def store(cols, value):
        o_ref[:, cols] = value

    _pool_mix(x_ref[...], s * x_ref.shape[0], carry_ref, mod_ref, gain_ref[...], pw_ref,
              ps_ref[...], store)


def _pool_mixer(x, mods, gain, pool_w, pool_scale, layer, j_layer):
    batch, seq, d = x.shape
    ts = POOL_TOKENS
    n_groups = len(POOL_WINDOWS)
    est = (4 * ts * d * 4) + 4 * ts * d * 4 + 2 * n_groups * GROUP_DIM * GROUP_DIM * 2
    return pl.pallas_call(
        _pool_kernel,
        grid=(batch, seq // ts),
        in_specs=[
            pl.BlockSpec((None, ts, d), lambda b, s: (b, s, 0)),
            pl.BlockSpec((None, N_MOD, d), lambda b, s: (b, 0, 0)),
            pl.BlockSpec((None, 1, d), lambda b, s: (layer, 0, 0)),
            pl.BlockSpec((None, n_groups, GROUP_DIM, GROUP_DIM), lambda b, s: (j_layer, 0, 0, 0)),
            pl.BlockSpec((None, 1, d), lambda b, s: (j_layer, 0, 0)),
        ],
        out_specs=pl.BlockSpec((None, ts, d), lambda b, s: (b, s, 0)),
        out_shape=jax.ShapeDtypeStruct(x.shape, F32),
        scratch_shapes=[pltpu.VMEM((POOL_HALO, d), F32)],
        compiler_params=_params(2, est),
        name=f"pool_mixer_l{layer}",
    )(x, mods, gain, pool_w, pool_scale)


def kernel(x, c, norm_ffn1, norm_mix, norm_ffn2, w_ada, b_ada, w_ffn1_in, w_ffn1_out,
           w_ffn2_in, w_ffn2_out, conv_in, conv_w, conv_out, pool_w, pool_scale, final_norm):
    batch, seq, d = x.shape
    assert (batch, seq, d) == (c.shape[0], seq, D_MODEL) and seq % FFN_TOKENS == 0

    mods_all = _ada_mods(c, w_ada, b_ada).reshape(DEPTH, batch, N_MOD, d)

    gain1 = norm_ffn1.reshape(DEPTH, 1, d)
    gain_mix = norm_mix.reshape(DEPTH, 1, d)
    gain2 = norm_ffn2.reshape(DEPTH, 1, d)
    p_w = pool_w.astype(BF16)
    p_scale = pool_scale.reshape(pool_scale.shape[0], 1, d)
    f_gain = final_norm.reshape(1, d)

    w_in, w_out = w_ffn1_in[0].astype(BF16), w_ffn1_out[0].astype(BF16)
    for i in range(DEPTH):
        mods = mods_all[i]
        j = i // N_MIXERS
        is_conv = i % N_MIXERS == 0
        cast = [(w_ffn2_in, i), (w_ffn2_out, i)]
        if is_conv:
            cast += [(conv_in, j), (conv_out, j)]
        x, cast_out = _ffn(x, mods, gain1, w_in, w_out, i, 0, cast=cast)
        w_in, w_out = cast_out[:2]
        if is_conv:
            x = _conv_mixer(x, mods, gain_mix, cast_out[2], conv_w, cast_out[3], i, j)
        else:
            x = _pool_mixer(x, mods, gain_mix, p_w, p_scale, i, j)
        if i == DEPTH - 1:
            x, _ = _ffn(x, mods, gain2, w_in, w_out, i, 6, final_gain=f_gain)
        else:
            cast = [(w_ffn1_in, i + 1), (w_ffn1_out, i + 1)]
            x, (w_in, w_out) = _ffn(x, mods, gain2, w_in, w_out, i, 6, cast=cast)
    return x
```

```python
import functools

import jax
import jax.numpy as jnp
from jax import lax
from jax.experimental import pallas as pl
from jax.experimental.pallas import tpu as pltpu

D_MODEL = 2048
DEPTH = 4
N_MIXERS = 2
CONV_WIDTH = 3
POOL_WINDOWS = (2, 4, 8, 16)
GROUP_DIM = D_MODEL // len(POOL_WINDOWS)
D_FF = 5632
N_MOD = 9
EPS = 1e-6
FFN_RES_WEIGHT = 0.5

F32 = jnp.float32
BF16 = jnp.bfloat16

SUBLANES_F32 = 8
SUBLANES_BF16 = 16
VMEM_BYTES_V7X = 64 * 1024 * 1024

FFN_TOKENS = 1024
FFN_HIDDEN = 512
FFN_ROW_CHUNK = 512
CONV_TOKENS = 1024
CONV_CHANNELS = 512
CONV_ROW_CHUNK = 512
ADA_SIDE_COLS = 128
POOL_TOKENS = 1024
ADA_COLS = 2048
ADA_ROWS = 16
POOL_HALO = 16
CONV_HALO = SUBLANES_F32


def _vmem_limit(estimate_bytes):
    return int(min(estimate_bytes * 3 // 2 + (4 << 20), VMEM_BYTES_V7X - (1 << 20)))


def _params(n_axes, estimate_bytes):
    return pltpu.CompilerParams(
        dimension_semantics=("arbitrary",) * n_axes,
        vmem_limit_bytes=_vmem_limit(estimate_bytes),
    )


def _rms(x, gain):
    ms = jnp.mean(x * x, axis=-1, keepdims=True)
    return (x * lax.rsqrt(ms + EPS)) * gain


def _norm_mod(x, gain, shift, scale):
    return _rms(x, gain) * (1.0 + scale) + shift


def _ada_block(c_ref, w_ref, b_ref, o_ref):
    c = c_ref[...]
    ca = (c * jax.nn.sigmoid(c)).astype(BF16)
    y = jnp.dot(ca, w_ref[...].astype(BF16), preferred_element_type=F32)
    o_ref[...] = y[: o_ref.shape[0], :] + b_ref[...]


def _ada_mods_first(c_pad, w_ada, b_ada3, batch):
    n_out = w_ada.shape[-1]
    est = 2 * D_MODEL * ADA_COLS * 4 + D_MODEL * ADA_COLS * 2
    return pl.pallas_call(
        _ada_block,
        grid=(n_out // ADA_COLS,),
        in_specs=[
            pl.BlockSpec((ADA_ROWS, D_MODEL), lambda n: (0, 0)),
            pl.BlockSpec((None, D_MODEL, ADA_COLS), lambda n: (0, 0, n)),
            pl.BlockSpec((None, 1, ADA_COLS), lambda n: (0, 0, n)),
        ],
        out_specs=pl.BlockSpec((batch, ADA_COLS), lambda n: (0, n)),
        out_shape=jax.ShapeDtypeStruct((batch, n_out), F32),
        compiler_params=_params(1, est),
        name="ada_mods_l0",
    )(c_pad, w_ada, b_ada3)


def _ffn_kernel(x_ref, mod_ref, gain_ref, wg_ref, wu_ref, wo_ref, *rest, mod_row, final, ada,
                n_cast):
    if final:
        fgain_ref, rest = rest[0], rest[1:]
    if ada:
        ada_in, rest = rest[:3], rest[3:]
    cast_in, rest = rest[:n_cast], rest[n_cast:]
    o_ref, rest = rest[0], rest[1:]
    if ada:
        ada_out, rest = rest[0], rest[1:]
    cast_out, h_ref = rest[:n_cast], rest[n_cast]
    f = pl.program_id(2)
    ts = x_ref.shape[0]

    def side_jobs():
        for src, dst in zip(cast_in, cast_out):
            dst[...] = src[...].astype(BF16)
        if ada:
            _ada_block(*ada_in, ada_out)

    def swiglu_partial(h):
        g = jnp.dot(h, wg_ref[...], preferred_element_type=F32)
        u = jnp.dot(h, wu_ref[...], preferred_element_type=F32)
        act = ((g * jax.nn.sigmoid(g)) * u).astype(BF16)
        return jnp.dot(act, wo_ref[...], preferred_element_type=F32)

    @pl.when(f == 0)
    def _():
        shift = mod_ref[mod_row : mod_row + 1, :]
        scale = mod_ref[mod_row + 1 : mod_row + 2, :]
        gain = gain_ref[...]
        side_jobs()
        for r in range(ts // FFN_ROW_CHUNK):
            rows = pl.ds(r * FFN_ROW_CHUNK, FFN_ROW_CHUNK)
            h = _norm_mod(x_ref[rows, :], gain, shift, scale).astype(BF16)
            h_ref[rows, :] = h
            o_ref[rows, :] = swiglu_partial(h)

    last = pl.num_programs(2) - 1

    @pl.when(jnp.logical_and(f > 0, f < last))
    def _():
        side_jobs()
        o_ref[...] += swiglu_partial(h_ref[...])

    @pl.when(f == last)
    def _():
        gate = FFN_RES_WEIGHT * mod_ref[mod_row + 2 : mod_row + 3, :]
        side_jobs()
        for r in range(ts // FFN_ROW_CHUNK):
            rows = pl.ds(r * FFN_ROW_CHUNK, FFN_ROW_CHUNK)
            y = x_ref[rows, :] + gate * (o_ref[rows, :] + swiglu_partial(h_ref[rows, :]))
            if final:
                y = _rms(y, fgain_ref[...])
            o_ref[rows, :] = y


def _cast_block_rows(n_rows, n_steps):
    rows = SUBLANES_BF16
    while n_rows % rows or n_rows // rows > n_steps:
        rows += SUBLANES_BF16
    return rows


def _ffn(x, mods, gain, w_in, w_out, layer, mod_row, final_gain=None, cast=(), ada=None):
    batch, seq, d = x.shape
    ts, tf = FFN_TOKENS, FFN_HIDDEN
    nf = D_FF // tf
    ns = seq // ts
    n_steps = batch * ns * nf
    final = final_gain is not None

    def step_block(n_blocks):
        return lambda b, s, f: jnp.minimum((b * ns + s) * nf + f, n_blocks - 1)

    in_specs = [
        pl.BlockSpec((None, ts, d), lambda b, s, f: (b, s, 0)),
        pl.BlockSpec((None, N_MOD, d), lambda b, s, f: (b, 0, 0)),
        pl.BlockSpec((None, 1, d), lambda b, s, f: (layer, 0, 0), pipeline_mode=pl.Buffered(1)),
        pl.BlockSpec((d, tf), lambda b, s, f: (0, f)),
        pl.BlockSpec((d, tf), lambda b, s, f: (0, nf + f)),
        pl.BlockSpec((tf, d), lambda b, s, f: (f, 0)),
    ]
    args = [x, mods, gain, w_in, w_in, w_out]
    if final:
        in_specs.append(pl.BlockSpec((1, d), lambda b, s, f: (0, 0)))
        args.append(final_gain)
    out_specs = [pl.BlockSpec((None, ts, d), lambda b, s, f: (b, s, 0))]
    out_shape = [jax.ShapeDtypeStruct(x.shape, F32)]
    est = (4 * ts * d * 4) + ts * d * 2 + 2 * (3 * d * tf * 2) + 3 * ts * tf * 4
    if ada is not None:
        c_pad, w_ada, b_ada3, ada_layer = ada
        n_out = w_ada.shape[-1]
        assert n_out % ADA_SIDE_COLS == 0 and n_out // ADA_SIDE_COLS <= n_steps
        ada_blk = step_block(n_out // ADA_SIDE_COLS)
        in_specs += [
            pl.BlockSpec((ADA_ROWS, d), lambda b, s, f: (0, 0)),
            pl.BlockSpec((None, d, ADA_SIDE_COLS), lambda b, s, f: (ada_layer, 0, ada_blk(b, s, f))),
            pl.BlockSpec((None, 1, ADA_SIDE_COLS), lambda b, s, f: (ada_layer, 0, ada_blk(b, s, f))),
        ]
        args += [c_pad, w_ada, b_ada3]
        out_specs.append(pl.BlockSpec((batch, ADA_SIDE_COLS), lambda b, s, f: (0, ada_blk(b, s, f))))
        out_shape.append(jax.ShapeDtypeStruct((batch, n_out), F32))
        est += 2 * d * ADA_SIDE_COLS * 4 + d * ADA_SIDE_COLS * 2
    for w, w_layer in cast:
        _, n_rows, n_cols = w.shape
        rows = _cast_block_rows(n_rows, n_steps)
        blk = step_block(n_rows // rows)
        in_specs.append(pl.BlockSpec(
            (None, rows, n_cols), lambda b, s, f, w_layer=w_layer, blk=blk: (w_layer, blk(b, s, f), 0)))
        args.append(w)
        out_specs.append(pl.BlockSpec((rows, n_cols), lambda b, s, f, blk=blk: (blk(b, s, f), 0)))
        out_shape.append(jax.ShapeDtypeStruct((n_rows, n_cols), BF16))
        est += 2 * rows * n_cols * (4 + 2)
    outs = pl.pallas_call(
        functools.partial(_ffn_kernel, mod_row=mod_row, final=final, ada=ada is not None,
                          n_cast=len(cast)),
        grid=(batch, ns, nf),
        in_specs=in_specs,
        out_specs=out_specs,
        out_shape=out_shape,
        scratch_shapes=[pltpu.VMEM((ts, d), BF16)],
        compiler_params=_params(3, est),
        name=f"ffn_l{layer}_r{mod_row}",
    )(*args)
    n_lead = 2 if ada is not None else 1
    return outs[0], list(outs[n_lead:]), (outs[1] if ada is not None else None)


def _conv_kernel(x_ref, mod_ref, gain_ref, wb_ref, wc_ref, wv_ref, cw_ref, wo_ref,
                 o_ref, h_ref, u_ref, carry_ref):
    s = pl.program_id(1)
    j = pl.program_id(2)
    last = pl.num_programs(2) - 1
    ts = x_ref.shape[0]

    @pl.when(s == 0)
    def _():
        carry_ref[j] = jnp.zeros(carry_ref.shape[1:], F32)

    def mixer_partial(h, row0, n_rows):
        b_gate = jnp.dot(h, wb_ref[...], preferred_element_type=F32)
        c_gate = jnp.dot(h, wc_ref[...], preferred_element_type=F32)
        v = jnp.dot(h, wv_ref[...], preferred_element_type=F32)
        u = c_gate * v
        u_ref[pl.ds(CONV_HALO + row0, n_rows), :] = u
        y = cw_ref[CONV_WIDTH - 1 : CONV_WIDTH, :] * u
        for k in range(1, CONV_WIDTH):
            tap = cw_ref[CONV_WIDTH - 1 - k : CONV_WIDTH - k, :]
            y = y + tap * u_ref[pl.ds(CONV_HALO + row0 - k, n_rows), :]
        z = (b_gate * y).astype(BF16)
        return jnp.dot(z, wo_ref[...], preferred_element_type=F32)

    u_ref[0:CONV_HALO, :] = carry_ref[j]
    row_chunks = [(r * CONV_ROW_CHUNK, CONV_ROW_CHUNK) for r in range(ts // CONV_ROW_CHUNK)]

    @pl.when(j == 0)
    def _():
        shift = mod_ref[3:4, :]
        scale = mod_ref[4:5, :]
        gain = gain_ref[...]
        for row0, n in row_chunks:
            rows = pl.ds(row0, n)
            h = _norm_mod(x_ref[rows, :], gain, shift, scale).astype(BF16)
            h_ref[rows, :] = h
            o_ref[rows, :] = mixer_partial(h, row0, n)

    @pl.when(jnp.logical_and(j > 0, j < last))
    def _():
        o_ref[...] += mixer_partial(h_ref[...], 0, ts)

    @pl.when(j == last)
    def _():
        gate = mod_ref[5:6, :]
        for row0, n in row_chunks:
            rows = pl.ds(row0, n)
            part = mixer_partial(h_ref[rows, :], row0, n)
            o_ref[rows, :] = x_ref[rows, :] + gate * (o_ref[rows, :] + part)

    carry_ref[j] = u_ref[pl.ds(ts, CONV_HALO), :]


def _conv_mixer(x, mods, gain, w_in, w_conv, w_out, layer, j_layer):
    batch, seq, d = x.shape
    ts, tc = CONV_TOKENS, CONV_CHANNELS
    nc = d // tc
    est = (4 * ts * d * 4) + ts * d * 2 + 2 * (4 * d * tc * 2) + 5 * ts * tc * 4
    return pl.pallas_call(
        _conv_kernel,
        grid=(batch, seq // ts, nc),
        in_specs=[
            pl.BlockSpec((None, ts, d), lambda b, s, j: (b, s, 0)),
            pl.BlockSpec((None, N_MOD, d), lambda b, s, j: (b, 0, 0)),
            pl.BlockSpec((None, 1, d), lambda b, s, j: (layer, 0, 0)),
            pl.BlockSpec((d, tc), lambda b, s, j: (0, j)),
            pl.BlockSpec((d, tc), lambda b, s, j: (0, nc + j)),
            pl.BlockSpec((d, tc), lambda b, s, j: (0, 2 * nc + j)),
            pl.BlockSpec((None, CONV_WIDTH, tc), lambda b, s, j: (j_layer, 0, j)),
            pl.BlockSpec((tc, d), lambda b, s, j: (j, 0)),
        ],
        out_specs=pl.BlockSpec((None, ts, d), lambda b, s, j: (b, s, 0)),
        out_shape=jax.ShapeDtypeStruct(x.shape, F32),
        scratch_shapes=[
            pltpu.VMEM((ts, d), BF16),
            pltpu.VMEM((CONV_HALO + ts, tc), F32),
            pltpu.VMEM((nc, CONV_HALO, tc), F32),
        ],
        compiler_params=_params(3, est),
        name=f"conv_mixer_l{layer}",
    )(x, mods, gain, w_in, w_in, w_in, w_conv, w_out)


def _pool_kernel(x_ref, mod_ref, gain_ref, pw_ref, ps_ref, o_ref, carry_ref):
    s = pl.program_id(1)
    ts = x_ref.shape[0]

    @pl.when(s == 0)
    def _():
        carry_ref[...] = jnp.zeros_like(carry_ref)

    x = x_ref[...]
    h = _norm_mod(x, gain_ref[...], mod_ref[3:4, :], mod_ref[4:5, :])
    gate = mod_ref[5:6, :]
    scale = ps_ref[...]
    pos = s * ts + lax.broadcasted_iota(jnp.int32, (ts, 1), 0)
    for g, w in enumerate(POOL_WINDOWS):
        cols = slice(g * GROUP_DIM, (g + 1) * GROUP_DIM)
        hg = h[:, cols]
        acc = jnp.concatenate([carry_ref[:, cols], hg], axis=0)
        shift = 1
        while shift < w:
            acc = acc + pltpu.roll(acc, shift, axis=0)
            shift *= 2
        win_sum = acc[POOL_HALO:, :]
        inv_count = 1.0 / jnp.minimum(pos + 1, w).astype(F32)
        pooled = (win_sum * inv_count - hg).astype(BF16)
        y = jnp.dot(pooled, pw_ref[g], preferred_element_type=F32)
        o_ref[:, cols] = x[:, cols] + gate[:, cols] * (y * scale[:, cols])
    carry_ref[...] = h[ts - POOL_HALO :, :]


def _pool_mixer(x, mods, gain, pool_w, pool_scale, layer, j_layer):
    batch, seq, d = x.shape
    ts = POOL_TOKENS
    n_groups = len(POOL_WINDOWS)
    est = (4 * ts * d * 4) + 4 * ts * d * 4 + 2 * n_groups * GROUP_DIM * GROUP_DIM * 2
    return pl.pallas_call(
        _pool_kernel,
        grid=(batch, seq // ts),
        in_specs=[
            pl.BlockSpec((None, ts, d), lambda b, s: (b, s, 0)),
            pl.BlockSpec((None, N_MOD, d), lambda b, s: (b, 0, 0)),
            pl.BlockSpec((None, 1, d), lambda b, s: (layer, 0, 0)),
            pl.BlockSpec((None, n_groups, GROUP_DIM, GROUP_DIM), lambda b, s: (j_layer, 0, 0, 0),
                         pipeline_mode=pl.Buffered(1)),
            pl.BlockSpec((None, 1, d), lambda b, s: (j_layer, 0, 0)),
        ],
        out_specs=pl.BlockSpec((None, ts, d), lambda b, s: (b, s, 0)),
        out_shape=jax.ShapeDtypeStruct(x.shape, F32),
        scratch_shapes=[pltpu.VMEM((POOL_HALO, d), F32)],
        compiler_params=_params(2, est),
        name=f"pool_mixer_l{layer}",
    )(x, mods, gain, pool_w, pool_scale)


def kernel(x, c, norm_ffn1, norm_mix, norm_ffn2, w_ada, b_ada, w_ffn1_in, w_ffn1_out,
           w_ffn2_in, w_ffn2_out, conv_in, conv_w, conv_out, pool_w, pool_scale, final_norm):
    batch, seq, d = x.shape
    assert (batch, seq, d) == (c.shape[0], seq, D_MODEL) and seq % FFN_TOKENS == 0

    c_pad = jnp.pad(c, ((0, ADA_ROWS - batch), (0, 0)))
    b_ada3 = b_ada.reshape(DEPTH, 1, N_MOD * d)
    mods = _ada_mods_first(c_pad, w_ada, b_ada3, batch).reshape(batch, N_MOD, d)

    gain1 = norm_ffn1.reshape(DEPTH, 1, d)
    gain_mix = norm_mix.reshape(DEPTH, 1, d)
    gain2 = norm_ffn2.reshape(DEPTH, 1, d)
    p_w = pool_w.astype(BF16)
    p_scale = pool_scale.reshape(pool_scale.shape[0], 1, d)
    f_gain = final_norm.reshape(1, d)

    w_in, w_out = w_ffn1_in[0].astype(BF16), w_ffn1_out[0].astype(BF16)
    for i in range(DEPTH):
        j = i // N_MIXERS
        is_conv = i % N_MIXERS == 0
        cast = [(w_ffn2_in, i), (w_ffn2_out, i)]
        if is_conv:
            cast += [(conv_in, j), (conv_out, j)]
        x, cast_out, _ = _ffn(x, mods, gain1, w_in, w_out, i, 0, cast=cast)
        w_in, w_out = cast_out[:2]
        if is_conv:
            x = _conv_mixer(x, mods, gain_mix, cast_out[2], conv_w, cast_out[3], i, j)
        else:
            x = _pool_mixer(x, mods, gain_mix, p_w, p_scale, i, j)
        if i == DEPTH - 1:
            x, _, _ = _ffn(x, mods, gain2, w_in, w_out, i, 6, final_gain=f_gain)
        else:
            cast = [(w_ffn1_in, i + 1), (w_ffn1_out, i + 1)]
            x, (w_in, w_out), next_mods = _ffn(x, mods, gain2, w_in, w_out, i, 6, cast=cast,
                                               ada=(c_pad, w_ada, b_ada3, i + 1))
            mods = next_mods.reshape(batch, N_MOD, d)
    return x
```
